```python
import jax, jax.numpy as jnp
from jax import lax
import numpy as np

D_MODEL = 2048
BATCH = 1
SEQ = 8192
DEPTH = 2
DEC_BATCH = 128
DEC_SEQ = 1
PAST_LEN = 2048
PAGE_SIZE = 128

HEAD_DIM = 128
N_GROUPS = 3
GROUP_WINDOWS = (128, 512, 2048)
GROUP_DILATIONS = (1, 4, 16)
HEADS_PER_GROUP = 4
N_ATTN_HEADS = N_GROUPS * HEADS_PER_GROUP
ATTN_WIDTH = N_ATTN_HEADS * HEAD_DIM
ATTN_OUT_WIDTH = HEADS_PER_GROUP * HEAD_DIM
D_RNN = 3 * D_MODEL // 4
RNN_BLOCKS = 12
RNN_BLOCK = D_RNN // RNN_BLOCKS
RNN_CONV_WIDTH = 4
LRU_C = 8.0
D_CONV = 3 * D_MODEL // 4
CONF_KERNEL = 31
N_BRANCHES = 3
D_FF = 4 * D_MODEL
N_IN = 3 * ATTN_WIDTH + D_RNN + 2 * D_CONV + N_BRANCHES * D_MODEL
Q_BLOCK = 128
NORM_EPS = 1e-6
NEG_INF = -1e30

kernel_name = "hybrid_dilated_rglru_conformer_decode_step"


def _alibi_slopes():
    h = np.arange(1, N_ATTN_HEADS + 1, dtype=np.float32)
    return np.power(np.float32(2.0), -8.0 * h / N_ATTN_HEADS).astype(np.float32)


def rmsnorm(x, g):
    xf = x.astype(jnp.float32)
    y = xf * lax.rsqrt(jnp.mean(xf * xf, axis=-1, keepdims=True) + NORM_EPS)
    return (y * g.astype(jnp.float32)).astype(x.dtype)


def layernorm(x, g, b):
    xf = x.astype(jnp.float32)
    mu = jnp.mean(xf, axis=-1, keepdims=True)
    var = jnp.mean(jnp.square(xf - mu), axis=-1, keepdims=True)
    y = (xf - mu) * lax.rsqrt(var + NORM_EPS)
    return (y * g.astype(jnp.float32) + b.astype(jnp.float32)).astype(x.dtype)


def causal_dwconv(x_ext, w, b):
    c = x_ext.shape[-1]
    y = lax.conv_general_dilated(x_ext, w[:, None, :].astype(x_ext.dtype), window_strides=(1,),
                                 padding='VALID', dimension_numbers=('NWC', 'WIO', 'NWC'),
                                 feature_group_count=c)
    return y + b.astype(y.dtype)


def block_diag(x, w, b):
    xb = x.reshape(x.shape[:-1] + (RNN_BLOCKS, RNN_BLOCK))
    y = jnp.einsum('...nj,njk->...nk', xb, w)
    return y.reshape(x.shape) + b


def rg_lru(xc, h0, wa, ba, wx, bx, lam):
    r = jax.nn.sigmoid(block_diag(xc, wa, ba).astype(jnp.float32))
    i = jax.nn.sigmoid(block_diag(xc, wx, bx).astype(jnp.float32))
    log_a = -LRU_C * r * jax.nn.softplus(-lam.astype(jnp.float32))
    a = jnp.exp(log_a)
    b = jnp.sqrt(-jnp.expm1(2.0 * log_a)) * (i * xc.astype(jnp.float32))

    def combine(c1, c2):
        a1, b1 = c1
        a2, b2 = c2
        return a1 * a2, a2 * b1 + b2

    a_cum, b_cum = lax.associative_scan(combine, (a, b), axis=1)
    return a_cum * h0.astype(jnp.float32)[:, None, :] + b_cum


def group_attend(q, kv_src, q_idx, slopes, dilation, n_keys):
    dist = dilation * jnp.arange(n_keys, dtype=jnp.int32)
    idx = q_idx[:, None] - dist[None, :]
    valid = idx >= 0
    kv = jnp.take(kv_src, jnp.maximum(idx, 0), axis=1).astype(jnp.float32)
    s = jnp.einsum('bqhd,bqkhd->bqhk', q, kv[:, :, :, 0])
    s = s - slopes[:, None] * dist.astype(jnp.float32)[None, :]
    s = jnp.where(valid[None, :, None, :], s, NEG_INF)
    lse = jax.nn.logsumexp(s, axis=-1)
    p = jnp.exp(s - lse[..., None])
    o = jnp.einsum('bqhk,bqkhd->bqhd', p, kv[:, :, :, 1])
    return o, lse


def dilated_mixture(q, kv_srcs, q_idxs):
    slopes = jnp.asarray(_alibi_slopes()).reshape(N_GROUPS, HEADS_PER_GROUP)
    outs, lses = [], []
    for g in range(N_GROUPS):
        o, lse = group_attend(q[:, :, g], kv_srcs[g], q_idxs[g], slopes[g], GROUP_DILATIONS[g],
                              GROUP_WINDOWS[g] // GROUP_DILATIONS[g] + 1)
        outs.append(o)
        lses.append(lse)
    o = jnp.stack(outs, axis=2)
    w = jax.nn.softmax(jnp.stack(lses, axis=2), axis=2)
    merged = jnp.einsum('bqgh,bqghd->bqhd', w, o)
    return merged.reshape(q.shape[0], q.shape[1], ATTN_OUT_WIDTH)


def prompt_attention(q, kvs):
    b, s = q.shape[:2]
    nb = s // Q_BLOCK
    qb = jnp.moveaxis(q.reshape(b, nb, Q_BLOCK, N_GROUPS, HEADS_PER_GROUP, HEAD_DIM), 1, 0)
    idxb = jnp.arange(s, dtype=jnp.int32).reshape(nb, Q_BLOCK)

    def body(args):
        q_blk, idx_blk = args
        return dilated_mixture(q_blk, kvs, (idx_blk,) * N_GROUPS)

    out = lax.map(body, (qb, idxb))
    return jnp.moveaxis(out, 0, 1).reshape(b, s, ATTN_OUT_WIDTH)


def trunk_layer(x, p, past):
    (norm1_g, w_in, b_in, lru_conv_w, lru_conv_b, lru_wa, lru_ba, lru_wx, lru_bx, lru_lambda,
     cc_dw_w, cc_dw_b, cc_ln_g, cc_ln_b, w_proj_a, w_proj_b, w_proj_c, w_out,
     norm2_g, w_up, w_down) = p
    b, t, _ = x.shape
    h = rmsnorm(x, norm1_g)
    z = jnp.einsum('btd,dn->btn', h, w_in) + b_in
    cuts = [ATTN_WIDTH, 3 * ATTN_WIDTH, 3 * ATTN_WIDTH + D_RNN, 3 * ATTN_WIDTH + D_RNN + 2 * D_CONV]
    q, kv, xr, glu, gates = jnp.split(z, cuts, axis=-1)
    q = q.reshape(b, t, N_GROUPS, HEADS_PER_GROUP, HEAD_DIM).astype(jnp.float32) * (HEAD_DIM ** -0.5)
    kv = kv.reshape(b, t, 2, N_GROUPS, HEADS_PER_GROUP, HEAD_DIM)
    kv_new = tuple(kv[:, :, :, g] for g in range(N_GROUPS))

    if past is None:
        attn = prompt_attention(q, kv_new)
        kv_state = tuple(kv_new[g][:, -min(GROUP_WINDOWS[g], t):] for g in range(N_GROUPS))
        h0 = jnp.zeros((b, D_RNN), jnp.float32)
        lru_pad = jnp.zeros((b, RNN_CONV_WIDTH - 1, D_RNN), xr.dtype)
        cc_pad = jnp.zeros((b, CONF_KERNEL - 1, D_CONV), glu.dtype)
    else:
        kv_bufs, h0, lru_pad, cc_pad = past
        srcs = tuple(jnp.concatenate([kv_bufs[g].astype(kv_new[g].dtype), kv_new[g]], axis=1)
                     for g in range(N_GROUPS))
        idxs = tuple(kv_bufs[g].shape[1] + jnp.arange(t, dtype=jnp.int32) for g in range(N_GROUPS))
        attn = dilated_mixture(q, srcs, idxs)
        kv_state = kv_new

    xr_ext = jnp.concatenate([lru_pad.astype(xr.dtype), xr], axis=1)
    xc = causal_dwconv(xr_ext, lru_conv_w, lru_conv_b)
    hs = rg_lru(xc, h0, lru_wa, lru_ba, lru_wx, lru_bx, lru_lambda)
    lru_h_state = hs[:, -1]
    lru_conv_state = xr_ext[:, -(RNN_CONV_WIDTH - 1):]

    u = glu[..., :D_CONV] * jax.nn.sigmoid(glu[..., D_CONV:])
    u_ext = jnp.concatenate([cc_pad.astype(u.dtype), u], axis=1)
    cv = jax.nn.silu(layernorm(causal_dwconv(u_ext, cc_dw_w, cc_dw_b), cc_ln_g, cc_ln_b))
    cc_state = u_ext[:, -(CONF_KERNEL - 1):]

    gate = jax.nn.sigmoid(gates.reshape(b, t, N_BRANCHES, D_MODEL).astype(jnp.float32))
    ya = jnp.einsum('btc,cd->btd', attn.astype(x.dtype), w_proj_a)
    yb = jnp.einsum('btc,cd->btd', hs.astype(x.dtype), w_proj_b)
    yc = jnp.einsum('btc,cd->btd', cv, w_proj_c)
    merged = (gate[:, :, 0] * ya + gate[:, :, 1] * yb + gate[:, :, 2] * yc).astype(x.dtype)
    x = x + jnp.einsum('btc,cd->btd', merged, w_out)

    h2 = rmsnorm(x, norm2_g)
    f = jnp.square(jax.nn.relu(jnp.einsum('btd,df->btf', h2, w_up)))
    x = x + jnp.einsum('btf,fd->btd', f, w_down)
    return x, (kv_state[0], kv_state[1], kv_state[2], lru_h_state, lru_conv_state, cc_state)


def _stack(states, i):
    return jnp.stack([st[i] for st in states], axis=0)


def setup_inputs(seed: int = 0) -> dict:
    key = jax.random.key(seed)
    ks = iter(jax.random.split(key, 40))
    f32 = jnp.float32

    def nrm(shape, scale):
        return jax.random.normal(next(ks), shape, f32) * scale

    lens = [min(w, PAST_LEN) for w in GROUP_WINDOWS]
    x_prompt = nrm((BATCH, SEQ, D_MODEL), 1.0)
    x_sample = nrm((DEC_BATCH, DEC_SEQ, D_MODEL), 1.0)
    cache_kv_w128 = nrm((DEPTH, DEC_BATCH, lens[0], 2, HEADS_PER_GROUP, HEAD_DIM), 1.0)
    cache_kv_w512 = nrm((DEPTH, DEC_BATCH, lens[1], 2, HEADS_PER_GROUP, HEAD_DIM), 1.0)
    cache_kv_w2048 = nrm((DEPTH, DEC_BATCH, lens[2], 2, HEADS_PER_GROUP, HEAD_DIM), 1.0)
    state_lru_h = nrm((DEPTH, DEC_BATCH, D_RNN), 0.5)
    state_lru_conv = nrm((DEPTH, DEC_BATCH, RNN_CONV_WIDTH - 1, D_RNN), 1.0)
    state_cconv = nrm((DEPTH, DEC_BATCH, CONF_KERNEL - 1, D_CONV), 1.0)

    norm1_g = 1.0 + nrm((DEPTH, D_MODEL), 0.02)
    w_in = nrm((DEPTH, D_MODEL, N_IN), D_MODEL ** -0.5)
    b_in = nrm((DEPTH, N_IN), 0.02)
    lru_conv_w = nrm((DEPTH, RNN_CONV_WIDTH, D_RNN), RNN_CONV_WIDTH ** -0.5)
    lru_conv_b = nrm((DEPTH, D_RNN), 0.02)
    lru_wa = nrm((DEPTH, RNN_BLOCKS, RNN_BLOCK, RNN_BLOCK), RNN_BLOCK ** -0.5)
    lru_ba = nrm((DEPTH, D_RNN), 0.02)
    lru_wx = nrm((DEPTH, RNN_BLOCKS, RNN_BLOCK, RNN_BLOCK), RNN_BLOCK ** -0.5)
    lru_bx = nrm((DEPTH, D_RNN), 0.02)
    u = jax.random.uniform(next(ks), (DEPTH, D_RNN), f32, 0.9, 0.999)
    s = u ** (1.0 / LRU_C)
    lru_lambda = jnp.log(s) - jnp.log1p(-s)
    cc_dw_w = nrm((DEPTH, CONF_KERNEL, D_CONV), CONF_KERNEL ** -0.5)
    cc_dw_b = nrm((DEPTH, D_CONV), 0.02)
    cc_ln_g = 1.0 + nrm((DEPTH, D_CONV), 0.02)
    cc_ln_b = nrm((DEPTH, D_CONV), 0.02)
    w_proj_a = nrm((DEPTH, ATTN_OUT_WIDTH, D_MODEL), ATTN_OUT_WIDTH ** -0.5)
    w_proj_b = nrm((DEPTH, D_RNN, D_MODEL), D_RNN ** -0.5)
    w_proj_c = nrm((DEPTH, D_CONV, D_MODEL), D_CONV ** -0.5)
    w_out = nrm((DEPTH, D_MODEL, D_MODEL), D_MODEL ** -0.5)
    norm2_g = 1.0 + nrm((DEPTH, D_MODEL), 0.02)
    w_up = nrm((DEPTH, D_MODEL, D_FF), D_MODEL ** -0.5)
    w_down = nrm((DEPTH, D_FF, D_MODEL), D_FF ** -0.5)
    final_g = 1.0 + nrm((D_MODEL,), 0.02)
    return {"x_prompt": x_prompt, "x_sample": x_sample,
            "cache_kv_w128": cache_kv_w128, "cache_kv_w512": cache_kv_w512, "cache_kv_w2048": cache_kv_w2048,
            "state_lru_h": state_lru_h, "state_lru_conv": state_lru_conv, "state_cconv": state_cconv,
            "norm1_g": norm1_g, "w_in": w_in, "b_in": b_in,
            "lru_conv_w": lru_conv_w, "lru_conv_b": lru_conv_b, "lru_wa": lru_wa, "lru_ba": lru_ba,
            "lru_wx": lru_wx, "lru_bx": lru_bx, "lru_lambda": lru_lambda,
            "cc_dw_w": cc_dw_w, "cc_dw_b": cc_dw_b, "cc_ln_g": cc_ln_g, "cc_ln_b": cc_ln_b,
            "w_proj_a": w_proj_a, "w_proj_b": w_proj_b, "w_proj_c": w_proj_c, "w_out": w_out,
            "norm2_g": norm2_g, "w_up": w_up, "w_down": w_down, "final_g": final_g}


def reference(x_prompt, x_sample, cache_kv_w128, cache_kv_w512, cache_kv_w2048, state_lru_h, state_lru_conv,
              state_cconv, norm1_g, w_in, b_in, lru_conv_w, lru_conv_b, lru_wa, lru_ba, lru_wx, lru_bx,
              lru_lambda, cc_dw_w, cc_dw_b, cc_ln_g, cc_ln_b, w_proj_a, w_proj_b, w_proj_c, w_out,
              norm2_g, w_up, w_down, final_g):
    weights = (norm1_g, w_in, b_in, lru_conv_w, lru_conv_b, lru_wa, lru_ba, lru_wx, lru_bx, lru_lambda,
               cc_dw_w, cc_dw_b, cc_ln_g, cc_ln_b, w_proj_a, w_proj_b, w_proj_c, w_out,
               norm2_g, w_up, w_down)
    xp, xs = x_prompt, x_sample
    p_states, s_states = [], []
    for l in range(DEPTH):
        p_l = tuple(w[l] for w in weights)
        xp, st_p = trunk_layer(xp, p_l, None)
        past = ((cache_kv_w128[l], cache_kv_w512[l], cache_kv_w2048[l]), state_lru_h[l], state_lru_conv[l],
                state_cconv[l])
        xs, st_s = trunk_layer(xs, p_l, past)
        p_states.append(st_p)
        s_states.append(st_s)
    y_prompt = rmsnorm(xp, final_g)
    y_sample = rmsnorm(xs, final_g)
    return (y_prompt, y_sample,
            _stack(p_states, 0), _stack(p_states, 1), _stack(p_states, 2),
            _stack(p_states, 3), _stack(p_states, 4), _stack(p_states, 5),
            _stack(s_states, 0), _stack(s_states, 1), _stack(s_states, 2),
            _stack(s_states, 3), _stack(s_states, 4), _stack(s_states, 5))
```

```python
import functools

import numpy as np
import jax
import jax.numpy as jnp
from jax import lax
from jax.experimental import pallas as pl
from jax.experimental.pallas import tpu as pltpu

f32 = jnp.float32
bf16 = jnp.bfloat16

D_MODEL = 2048
HEAD_DIM = 128
N_GROUPS = 3
GROUP_WINDOWS = (128, 512, 2048)
GROUP_DILATIONS = (1, 4, 16)
HEADS_PER_GROUP = 4
GROUP_WIDTH = HEADS_PER_GROUP * HEAD_DIM
N_ATTN_HEADS = N_GROUPS * HEADS_PER_GROUP
KEYS_PER_QUERY = 128
ATTN_WIDTH = N_ATTN_HEADS * HEAD_DIM
D_RNN = 1536
RNN_BLOCKS = 12
RNN_BLOCK = 128
RNN_CONV_WIDTH = 4
LRU_C = 8.0
D_CONV = 1536
CONF_KERNEL = 31
D_FF = 4 * D_MODEL
NORM_EPS = 1e-6
NEG_INF = -1e30

COL_Q = 0
COL_K = ATTN_WIDTH
COL_V = 2 * ATTN_WIDTH
COL_XR = 3 * ATTN_WIDTH
COL_GLU_A = COL_XR + D_RNN
COL_GLU_B = COL_GLU_A + D_CONV
COL_GATE = COL_GLU_B + D_CONV
N_IN = COL_GATE + 3 * D_MODEL

LANES = 128
SUBLANES = 8
VMEM_LIMIT = 56 * 1024 * 1024


def _alibi_slopes():
    h = np.arange(1, N_ATTN_HEADS + 1, dtype=np.float32)
    return np.power(np.float32(2.0), -8.0 * h / N_ATTN_HEADS).astype(np.float32)


def _params(n_axes):
    return pltpu.CompilerParams(dimension_semantics=("arbitrary",) * n_axes, vmem_limit_bytes=VMEM_LIMIT)


def _rmsnorm(x, g):
    ms = jnp.mean(x * x, axis=-1, keepdims=True)
    return x * lax.rsqrt(ms + NORM_EPS) * g


def _norm_matmul_kernel(x_ref, g_ref, w_ref, b_ref, o_ref, h_ref):
    @pl.when(pl.program_id(1) == 0)
    def _():
        h_ref[...] = _rmsnorm(x_ref[...], g_ref[...]).astype(bf16)

    o_ref[...] = jnp.dot(h_ref[...], w_ref[...], preferred_element_type=f32) + b_ref[...]


def norm_matmul(x, g, w, b, layer, tm, tn):
    m, d = x.shape
    n = w.shape[-1]
    return pl.pallas_call(
        _norm_matmul_kernel,
        grid=(m // tm, n // tn),
        in_specs=[pl.BlockSpec((tm, d), lambda i, j: (i, 0)),
                  pl.BlockSpec((None, 1, d), lambda i, j: (layer, 0, 0)),
                  pl.BlockSpec((None, d, tn), lambda i, j: (layer, 0, j)),
                  pl.BlockSpec((None, 1, tn), lambda i, j: (layer, 0, j))],
        out_specs=pl.BlockSpec((tm, tn), lambda i, j: (i, j)),
        out_shape=jax.ShapeDtypeStruct((m, n), f32),
        scratch_shapes=[pltpu.VMEM((tm, d), bf16)],
        compiler_params=_params(2),
        name="norm_matmul",
    )(x, g, w, b)


def _split_kernel(q_ref, k_ref, v_ref, qd_ref, kd_ref, vd_ref, scr, *, dil, tm):
    n = tm // dil
    for src, dst, scale in ((q_ref, qd_ref, HEAD_DIM ** -0.5), (k_ref, kd_ref, None), (v_ref, vd_ref, None)):
        if dil == 1:
            val = src[...]
            if scale is not None:
                val = val * scale
            dst[0] = val.astype(bf16)
            continue
        for h in range(HEADS_PER_GROUP):
            scr[h] = src[:, h * HEAD_DIM:(h + 1) * HEAD_DIM]
        for r in range(dil):
            val = jnp.concatenate([scr[h, pl.ds(r, n, stride=dil), :] for h in range(HEADS_PER_GROUP)], axis=-1)
            if scale is not None:
                val = val * scale
            dst[r] = val.astype(bf16)


def split_classes(z, group, tm):
    t = z.shape[0]
    dil = GROUP_DILATIONS[group]
    cb = lambda col: (col + group * GROUP_WIDTH) // GROUP_WIDTH
    in_spec = lambda col: pl.BlockSpec((tm, GROUP_WIDTH), lambda i, c=cb(col): (i, c))
    out_spec = pl.BlockSpec((dil, tm // dil, GROUP_WIDTH), lambda i: (0, i, 0))
    out_sds = jax.ShapeDtypeStruct((dil, t // dil, GROUP_WIDTH), bf16)
    return pl.pallas_call(
        functools.partial(_split_kernel, dil=dil, tm=tm),
        grid=(t // tm,),
        in_specs=[in_spec(COL_Q), in_spec(COL_K), in_spec(COL_V)],
        out_specs=[out_spec] * 3,
        out_shape=[out_sds] * 3,
        scratch_shapes=[pltpu.VMEM((HEADS_PER_GROUP, tm, HEAD_DIM), f32)],
        compiler_params=_params(1),
        name=f"split_classes_g{group}",
    )(z, z, z)


def _band_attn_kernel(q_ref, kp_ref, kc_ref, vp_ref, vc_ref, o_ref, lse_ref, *, dil, slopes):
    c = pl.program_id(1)
    shape = (KEYS_PER_QUERY, 2 * KEYS_PER_QUERY)
    row = lax.broadcasted_iota(jnp.int32, shape, 0)
    col = lax.broadcasted_iota(jnp.int32, shape, 1)
    steps = row + KEYS_PER_QUERY - col
    first_valid_col = jnp.where(c > 0, 0, KEYS_PER_QUERY)
    valid = (steps >= 0) & (steps <= KEYS_PER_QUERY) & (col >= first_valid_col)
    dist = steps.astype(f32) * float(dil)
    for h in range(HEADS_PER_GROUP):
        sl = slice(h * HEAD_DIM, (h + 1) * HEAD_DIM)
        k2 = jnp.concatenate([kp_ref[:, sl], kc_ref[:, sl]], axis=0)
        v2 = jnp.concatenate([vp_ref[:, sl], vc_ref[:, sl]], axis=0)
        s = lax.dot_general(q_ref[:, sl], k2, (((1,), (1,)), ((), ())), preferred_element_type=f32)
        s = jnp.where(valid, s - slopes[h] * dist, NEG_INF)
        m = jnp.max(s, axis=-1, keepdims=True)
        p = jnp.exp(s - m)
        l = jnp.sum(p, axis=-1, keepdims=True)
        o_ref[:, sl] = jnp.dot(p.astype(bf16), v2, preferred_element_type=f32) / l
        lse_ref[:, sl] = jnp.broadcast_to(m + jnp.log(l), (KEYS_PER_QUERY, HEAD_DIM))


def band_attention(qd, kd, vd, group):
    dil, length, _ = qd.shape
    slopes = tuple(float(s) for s in _alibi_slopes()[group * HEADS_PER_GROUP:(group + 1) * HEADS_PER_GROUP])
    blk = (None, KEYS_PER_QUERY, GROUP_WIDTH)
    cur = pl.BlockSpec(blk, lambda r, c: (r, c, 0))
    prev = pl.BlockSpec(blk, lambda r, c: (r, jnp.maximum(c - 1, 0), 0))
    out_sds = jax.ShapeDtypeStruct((dil, length, GROUP_WIDTH), f32)
    return pl.pallas_call(
        functools.partial(_band_attn_kernel, dil=dil, slopes=slopes),
        grid=(dil, length // KEYS_PER_QUERY),
        in_specs=[cur, prev, cur, prev, cur],
        out_specs=[cur, cur],
        out_shape=[out_sds, out_sds],
        compiler_params=_params(2),
        name=f"band_attention_g{group}",
    )(qd, kd, kd, vd, vd)


def _merge_groups_kernel(o0_ref, l0_ref, o1_ref, l1_ref, o2_ref, l2_ref, out_ref, so1, sl1, so2, sl2, *, tm):
    for src, dst, dil in ((o1_ref, so1, GROUP_DILATIONS[1]), (l1_ref, sl1, GROUP_DILATIONS[1]),
                          (o2_ref, so2, GROUP_DILATIONS[2]), (l2_ref, sl2, GROUP_DILATIONS[2])):
        n = tm // dil
        for r in range(dil):
            for h in range(HEADS_PER_GROUP):
                dst[h, pl.ds(r, n, stride=dil), :] = src[r, :, h * HEAD_DIM:(h + 1) * HEAD_DIM]
    for h in range(HEADS_PER_GROUP):
        sl = slice(h * HEAD_DIM, (h + 1) * HEAD_DIM)
        lses = (l0_ref[0, :, sl], sl1[h], sl2[h])
        outs = (o0_ref[0, :, sl], so1[h], so2[h])
        m = jnp.maximum(jnp.maximum(lses[0], lses[1]), lses[2])
        e = [jnp.exp(l - m) for l in lses]
        num = e[0] * outs[0] + e[1] * outs[1] + e[2] * outs[2]
        out_ref[:, sl] = (num / (e[0] + e[1] + e[2])).astype(bf16)


def merge_groups(group_outs, tm):
    t = group_outs[0][0].shape[1]
    specs, args = [], []
    for g, (o, lse) in enumerate(group_outs):
        dil = GROUP_DILATIONS[g]
        spec = pl.BlockSpec((dil, tm // dil, GROUP_WIDTH), lambda i: (0, i, 0))
        specs += [spec, spec]
        args += [o, lse]
    scr = pltpu.VMEM((HEADS_PER_GROUP, tm, HEAD_DIM), f32)
    return pl.pallas_call(
        functools.partial(_merge_groups_kernel, tm=tm),
        grid=(t // tm,),
        in_specs=specs,
        out_specs=pl.BlockSpec((tm, GROUP_WIDTH), lambda i: (i, 0)),
        out_shape=jax.ShapeDtypeStruct((t, GROUP_WIDTH), bf16),
        scratch_shapes=[scr] * 4,
        compiler_params=_params(1),
        name="merge_groups",
    )(*args)


def _lru_coeffs(xc, wab, ba, bx, lam):
    gates = jnp.dot(xc.astype(bf16), wab, preferred_element_type=f32)
    r = jax.nn.sigmoid(gates[:, :RNN_BLOCK] + ba)
    i = jax.nn.sigmoid(gates[:, RNN_BLOCK:] + bx)
    softplus_neg_lam = jnp.maximum(-lam, 0.0) + jnp.log(1.0 + jnp.exp(-jnp.abs(lam)))
    log_a = -LRU_C * r * softplus_neg_lam
    a = jnp.exp(log_a)
    th = jnp.tanh(log_a)
    one_minus_a2 = -2.0 * th / (1.0 - th)
    b = jnp.sqrt(one_minus_a2) * (i * xc)
    return a, b


def _lru_prompt_kernel(x_ref, cw_ref, cb_ref, wab_ref, ba_ref, bx_ref, lam_ref, hs_ref, hl_ref,
                       xbuf, a_scr, b_scr, hcar, *, tb):
    @pl.when(pl.program_id(1) == 0)
    def _():
        xbuf[0:SUBLANES] = jnp.zeros((SUBLANES, RNN_BLOCK), f32)
        hcar[...] = jnp.zeros_like(hcar)

    xbuf[SUBLANES:SUBLANES + tb] = x_ref[...]
    cw = cw_ref[...]
    xc = cb_ref[...] + cw[3:4] * xbuf[pl.ds(SUBLANES, tb), :]
    for back in range(1, RNN_CONV_WIDTH):
        xc = xc + cw[3 - back:4 - back] * xbuf[pl.ds(SUBLANES - back, tb), :]
    a, b = _lru_coeffs(xc, wab_ref[...], ba_ref[...], bx_ref[...], lam_ref[...])

    a3 = a.reshape(tb // SUBLANES, SUBLANES, RNN_BLOCK)
    b3 = b.reshape(tb // SUBLANES, SUBLANES, RNN_BLOCK)
    sub = lax.broadcasted_iota(jnp.int32, a3.shape, 1)
    for s in (1, 2, 4):
        keep = sub >= s
        a_prev = pltpu.roll(a3, s, axis=1)
        b_prev = pltpu.roll(b3, s, axis=1)
        b3 = jnp.where(keep, a3 * b_prev + b3, b3)
        a3 = jnp.where(keep, a3 * a_prev, a3)
    a_scr[...] = a3.reshape(tb, RNN_BLOCK)
    b_scr[...] = b3.reshape(tb, RNN_BLOCK)

    def body(g, h_prev):
        rows = pl.ds(pl.multiple_of(g * SUBLANES, SUBLANES), SUBLANES)
        h = a_scr[rows, :] * h_prev + b_scr[rows, :]
        hs_ref[rows, :] = h.astype(hs_ref.dtype)
        return h[SUBLANES - 1:SUBLANES, :]

    h_last = lax.fori_loop(0, tb // SUBLANES, body, hcar[0:1, :])
    hcar[0:1, :] = h_last
    hl_ref[...] = h_last
    xbuf[0:SUBLANES] = xbuf[tb:tb + SUBLANES]


def lru_prompt(z, p, layer, tb):
    t = z.shape[0]
    xr_blk = COL_XR // RNN_BLOCK
    vec = lambda rows: pl.BlockSpec((None, rows, RNN_BLOCK), lambda n, i: (layer, 0, n))
    return pl.pallas_call(
        functools.partial(_lru_prompt_kernel, tb=tb),
        grid=(RNN_BLOCKS, t // tb),
        in_specs=[pl.BlockSpec((tb, RNN_BLOCK), lambda n, i: (i, xr_blk + n)),
                  vec(RNN_CONV_WIDTH), vec(1),
                  pl.BlockSpec((None, None, RNN_BLOCK, 2 * RNN_BLOCK), lambda n, i: (layer, n, 0, 0)),
                  vec(1), vec(1), vec(1)],
        out_specs=[pl.BlockSpec((tb, RNN_BLOCK), lambda n, i: (i, n)),
                   pl.BlockSpec((1, RNN_BLOCK), lambda n, i: (0, n))],
        out_shape=[jax.ShapeDtypeStruct((t, D_RNN), bf16), jax.ShapeDtypeStruct((1, D_RNN), f32)],
        scratch_shapes=[pltpu.VMEM((tb + SUBLANES, RNN_BLOCK), f32), pltpu.VMEM((tb, RNN_BLOCK), f32),
                        pltpu.VMEM((tb, RNN_BLOCK), f32), pltpu.VMEM((SUBLANES, RNN_BLOCK), f32)],
        compiler_params=_params(2),
        name="lru_prompt",
    )(z, p["lru_conv_w"], p["lru_conv_b"], p["lru_wab"], p["lru_ba"], p["lru_bx"], p["lru_lambda"])


CONF_HIST = 32


def _layernorm_swish_store(y_scr, g_ref, b_ref, cv_ref):
    n_blk = D_CONV // LANES
    total = y_scr[0].sum(axis=-1, keepdims=True)
    for k in range(1, n_blk):
        total = total + y_scr[k].sum(axis=-1, keepdims=True)
    mu = total / D_CONV
    sq = jnp.square(y_scr[0] - mu).sum(axis=-1, keepdims=True)
    for k in range(1, n_blk):
        sq = sq + jnp.square(y_scr[k] - mu).sum(axis=-1, keepdims=True)
    inv = lax.rsqrt(sq / D_CONV + NORM_EPS)
    for k in range(n_blk):
        sl = slice(k * LANES, (k + 1) * LANES)
        y = (y_scr[k] - mu) * inv * g_ref[:, sl] + b_ref[:, sl]
        cv_ref[:, sl] = (y * jax.nn.sigmoid(y)).astype(cv_ref.dtype)


def _conformer_prompt_kernel(ga_ref, gb_ref, w_ref, b_ref, lng_ref, lnb_ref, cv_ref, ulast_ref,
                             ubuf, hist, y_scr, *, tb):
    i = pl.program_id(0)
    n = pl.program_id(1)

    @pl.when(i == 0)
    def _():
        hist[n] = jnp.zeros((CONF_HIST, LANES), f32)

    u = ga_ref[...] * jax.nn.sigmoid(gb_ref[...])
    ubuf[0:CONF_HIST] = hist[n]
    ubuf[CONF_HIST:CONF_HIST + tb] = u
    hist[n] = u[tb - CONF_HIST:tb]
    ulast_ref[...] = u[tb - CONF_HIST:tb]
    first = CONF_HIST - (CONF_KERNEL - 1)
    acc = b_ref[...] + w_ref[0:1, :] * ubuf[pl.ds(first, tb), :]
    for j in range(1, CONF_KERNEL):
        acc = acc + w_ref[j:j + 1, :] * ubuf[pl.ds(first + j, tb), :]
    y_scr[n] = acc

    @pl.when(n == D_CONV // LANES - 1)
    def _():
        _layernorm_swish_store(y_scr, lng_ref, lnb_ref, cv_ref)


def conformer_prompt(z, p, layer, tb):
    t = z.shape[0]
    n_blk = D_CONV // LANES
    col = lambda base: pl.BlockSpec((tb, LANES), lambda i, n, c=base // LANES: (i, c + n))
    return pl.pallas_call(
        functools.partial(_conformer_prompt_kernel, tb=tb),
        grid=(t // tb, n_blk),
        in_specs=[col(COL_GLU_A), col(COL_GLU_B),
                  pl.BlockSpec((None, CONF_KERNEL, LANES), lambda i, n: (layer, 0, n)),
                  pl.BlockSpec((None, 1, LANES), lambda i, n: (layer, 0, n)),
                  pl.BlockSpec((None, 1, D_CONV), lambda i, n: (layer, 0, 0)),
                  pl.BlockSpec((None, 1, D_CONV), lambda i, n: (layer, 0, 0))],
        out_specs=[pl.BlockSpec((tb, D_CONV), lambda i, n: (i, 0)),
                   pl.BlockSpec((None, CONF_HIST, LANES), lambda i, n: (i, 0, n))],
        out_shape=[jax.ShapeDtypeStruct((t, D_CONV), bf16),
                   jax.ShapeDtypeStruct((t // tb, CONF_HIST, D_CONV), f32)],
        scratch_shapes=[pltpu.VMEM((CONF_HIST + tb, LANES), f32), pltpu.VMEM((n_blk, CONF_HIST, LANES), f32),
                        pltpu.VMEM((n_blk, tb, LANES), f32)],
        compiler_params=_params(2),
        name="conformer_prompt",
    )(z, z, p["cc_dw_w"], p["cc_dw_b"], p["cc_ln_g"], p["cc_ln_b"])


def _branch_merge_kernel(at_ref, hs_ref, cv_ref, wa_ref, wb_ref, wc_ref, ga_ref, gb_ref, gc_ref, o_ref):
    ya = jnp.dot(at_ref[...], wa_ref[...], preferred_element_type=f32)
    yb = jnp.dot(hs_ref[...], wb_ref[...], preferred_element_type=f32)
    yc = jnp.dot(cv_ref[...], wc_ref[...], preferred_element_type=f32)
    merged = (jax.nn.sigmoid(ga_ref[...]) * ya + jax.nn.sigmoid(gb_ref[...]) * yb
              + jax.nn.sigmoid(gc_ref[...]) * yc)
    o_ref[...] = merged.astype(o_ref.dtype)


def branch_merge(attn, hs, cv, z, p, layer, tm, tn):
    m = attn.shape[0]
    act = lambda width: pl.BlockSpec((tm, width), lambda i, j: (i, 0))
    wgt = lambda rows: pl.BlockSpec((None, rows, tn), lambda i, j: (layer, 0, j))
    gate = lambda br: pl.BlockSpec((tm, tn), lambda i, j, c=(COL_GATE + br * D_MODEL) // tn: (i, c + j))
    return pl.pallas_call(
        _branch_merge_kernel,
        grid=(m // tm, D_MODEL // tn),
        in_specs=[act(GROUP_WIDTH), act(D_RNN), act(D_CONV), wgt(GROUP_WIDTH), wgt(D_RNN), wgt(D_CONV),
                  gate(0), gate(1), gate(2)],
        out_specs=pl.BlockSpec((tm, tn), lambda i, j: (i, j)),
        out_shape=jax.ShapeDtypeStruct((m, D_MODEL), bf16),
        compiler_params=_params(2),
        name="branch_merge",
    )(attn, hs, cv, p["w_proj_a"], p["w_proj_b"], p["w_proj_c"], z, z, z)


def _residual_matmul_kernel(a_ref, w_ref, x_ref, o_ref):
    o_ref[...] = x_ref[...] + jnp.dot(a_ref[...], w_ref[...], preferred_element_type=f32)


def residual_matmul(a, w, x, layer, tm, tn):
    m, k = a.shape
    n = w.shape[-1]
    return pl.pallas_call(
        _residual_matmul_kernel,
        grid=(m // tm, n // tn),
        in_specs=[pl.BlockSpec((tm, k), lambda i, j: (i, 0)),
                  pl.BlockSpec((None, k, tn), lambda i, j: (layer, 0, j)),
                  pl.BlockSpec((tm, tn), lambda i, j: (i, j))],
        out_specs=pl.BlockSpec((tm, tn), lambda i, j: (i, j)),
        out_shape=jax.ShapeDtypeStruct((m, n), f32),
        compiler_params=_params(2),
        name="residual_matmul",
    )(a, w, x)


def _mlp_kernel(x_ref, g_ref, wu_ref, wd_ref, fg_ref, o_ref, h_ref, acc_ref, *, final_norm):
    f = pl.program_id(1)

    @pl.when(f == 0)
    def _():
        h_ref[...] = _rmsnorm(x_ref[...], g_ref[...]).astype(bf16)
        acc_ref[...] = jnp.zeros_like(acc_ref)

    up = jnp.dot(h_ref[...], wu_ref[...], preferred_element_type=f32)
    act = jnp.square(jnp.maximum(up, 0.0)).astype(bf16)
    acc_ref[...] += jnp.dot(act, wd_ref[...], preferred_element_type=f32)

    @pl.when(f == pl.num_programs(1) - 1)
    def _():
        y = x_ref[...] + acc_ref[...]
        if final_norm:
            y = _rmsnorm(y, fg_ref[...])
        o_ref[...] = y


def mlp(x, p, final_g, layer, tm, tf, final_norm):
    m, d = x.shape
    return pl.pallas_call(
        functools.partial(_mlp_kernel, final_norm=final_norm),
        grid=(m // tm, D_FF // tf),
        in_specs=[pl.BlockSpec((tm, d), lambda i, f: (i, 0)),
                  pl.BlockSpec((None, 1, d), lambda i, f: (layer, 0, 0)),
                  pl.BlockSpec((None, d, tf), lambda i, f: (layer, 0, f)),
                  pl.BlockSpec((None, tf, d), lambda i, f: (layer, f, 0)),
                  pl.BlockSpec((1, d), lambda i, f: (0, 0))],
        out_specs=pl.BlockSpec((tm, d), lambda i, f: (i, 0)),
        out_shape=jax.ShapeDtypeStruct((m, d), f32),
        scratch_shapes=[pltpu.VMEM((tm, d), bf16), pltpu.VMEM((tm, d), f32)],
        compiler_params=_params(2),
        name="mlp",
    )(x, p["norm2_g"], p["w_up"], p["w_down"], final_g)


def _sample_attn_kernel(qkv_ref, c0_ref, c1_ref, c2_ref, o_ref, *, slopes):
    bt = qkv_ref.shape[0]
    key_idx = lax.broadcasted_iota(jnp.int32, (KEYS_PER_QUERY, HEADS_PER_GROUP, 1), 0)
    head_idx = lax.broadcasted_iota(jnp.int32, (KEYS_PER_QUERY, HEADS_PER_GROUP, 1), 1)
    outs, lses = [], []
    for g, c_ref in enumerate((c0_ref, c1_ref, c2_ref)):
        dil = GROUP_DILATIONS[g]
        slope = jnp.zeros((KEYS_PER_QUERY, HEADS_PER_GROUP, 1), f32)
        for h in range(HEADS_PER_GROUP):
            slope = jnp.where(head_idx == h, slopes[g * HEADS_PER_GROUP + h], slope)
        bias = slope * ((KEYS_PER_QUERY - key_idx).astype(f32) * float(dil))
        q = qkv_ref[:, 0, g] * (HEAD_DIM ** -0.5)
        k_new = qkv_ref[:, 1, g]
        v_new = qkv_ref[:, 2, g]
        kc = c_ref[:, :, 0]
        vc = c_ref[:, :, 1]
        s_c = jnp.sum(q[:, None] * kc, axis=-1, keepdims=True) - bias[None]
        s_n = jnp.sum(q * k_new, axis=-1, keepdims=True)
        m = jnp.maximum(jnp.max(s_c, axis=1), s_n)
        p_c = jnp.exp(s_c - m[:, None])
        p_n = jnp.exp(s_n - m)
        l = jnp.sum(p_c, axis=1) + p_n
        o = (jnp.sum(p_c * vc, axis=1) + p_n * v_new) / l
        outs.append(o)
        lses.append(m + jnp.log(l))
    m = jnp.maximum(jnp.maximum(lses[0], lses[1]), lses[2])
    e = [jnp.exp(l - m) for l in lses]
    o_ref[...] = (e[0] * outs[0] + e[1] * outs[1] + e[2] * outs[2]) / (e[0] + e[1] + e[2])


def sample_attention(qkv, caches, layer, bt):
    b = qkv.shape[0]
    specs, args = [], []
    for g, cache in enumerate(caches):
        dil = GROUP_DILATIONS[g]
        depth, _, w = cache.shape[:3]
        assert w == KEYS_PER_QUERY * dil
        args.append(cache.reshape(depth, b, KEYS_PER_QUERY, dil, 2, HEADS_PER_GROUP, HEAD_DIM))
        specs.append(pl.BlockSpec((None, bt, KEYS_PER_QUERY, None, 2, HEADS_PER_GROUP, HEAD_DIM),
                                  lambda i: (layer, i, 0, 0, 0, 0, 0)))
    slopes = tuple(float(s) for s in _alibi_slopes())
    return pl.pallas_call(
        functools.partial(_sample_attn_kernel, slopes=slopes),
        grid=(b // bt,),
        in_specs=[pl.BlockSpec((bt, 3, N_GROUPS, HEADS_PER_GROUP, HEAD_DIM), lambda i: (i, 0, 0, 0, 0))] + specs,
        out_specs=pl.BlockSpec((bt, HEADS_PER_GROUP, HEAD_DIM), lambda i: (i, 0, 0)),
        out_shape=jax.ShapeDtypeStruct((b, HEADS_PER_GROUP, HEAD_DIM), f32),
        compiler_params=_params(1),
        name="sample_attention",
    )(qkv, *args)


def _sample_branches_kernel(xr_ref, lst_ref, h0_ref, cw_ref, cb_ref, wab_ref, ba_ref, bx_ref, lam_ref,
                            ga_ref, gb_ref, cst_ref, w_ref, b_ref, lng_ref, lnb_ref,
                            hs_ref, hn_ref, u_ref, cv_ref, y_scr):
    n = pl.program_id(0)
    cw = cw_ref[...]
    xc = cb_ref[...] + cw[3:4] * xr_ref[...]
    for j in range(RNN_CONV_WIDTH - 1):
        xc = xc + cw[j:j + 1] * lst_ref[j]
    a, b = _lru_coeffs(xc, wab_ref[...], ba_ref[...], bx_ref[...], lam_ref[...])
    h = a * h0_ref[...] + b
    hn_ref[...] = h
    hs_ref[...] = h.astype(hs_ref.dtype)

    u = ga_ref[...] * jax.nn.sigmoid(gb_ref[...])
    u_ref[...] = u
    acc = b_ref[...] + w_ref[CONF_KERNEL - 1:CONF_KERNEL, :] * u
    for j in range(CONF_KERNEL - 1):
        acc = acc + w_ref[j:j + 1, :] * cst_ref[j]
    y_scr[n] = acc

    @pl.when(n == D_CONV // LANES - 1)
    def _():
        _layernorm_swish_store(y_scr, lng_ref, lnb_ref, cv_ref)


def sample_branches(z, lru_state_t, h0, cc_state_t, p, layer):
    b = z.shape[0]
    n_blk = D_CONV // LANES
    col = lambda base: pl.BlockSpec((b, LANES), lambda n, c=base // LANES: (0, c + n))
    vec = lambda rows: pl.BlockSpec((None, rows, LANES), lambda n: (layer, 0, n))
    full = pl.BlockSpec((None, 1, D_CONV), lambda n: (layer, 0, 0))
    blk = pl.BlockSpec((b, LANES), lambda n: (0, n))
    return pl.pallas_call(
        _sample_branches_kernel,
        grid=(n_blk,),
        in_specs=[col(COL_XR),
                  pl.BlockSpec((None, RNN_CONV_WIDTH - 1, b, LANES), lambda n: (layer, 0, 0, n)),
                  pl.BlockSpec((None, b, LANES), lambda n: (layer, 0, n)),
                  vec(RNN_CONV_WIDTH), vec(1),
                  pl.BlockSpec((None, None, RNN_BLOCK, 2 * RNN_BLOCK), lambda n: (layer, n, 0, 0)),
                  vec(1), vec(1), vec(1),
                  col(COL_GLU_A), col(COL_GLU_B),
                  pl.BlockSpec((None, CONF_KERNEL - 1, b, LANES), lambda n: (layer, 0, 0, n)),
                  vec(CONF_KERNEL), vec(1), full, full],
        out_specs=[blk, blk, blk, pl.BlockSpec((b, D_CONV), lambda n: (0, 0))],
        out_shape=[jax.ShapeDtypeStruct((b, D_RNN), bf16), jax.ShapeDtypeStruct((b, D_RNN), f32),
                   jax.ShapeDtypeStruct((b, D_CONV), f32), jax.ShapeDtypeStruct((b, D_CONV), bf16)],
        scratch_shapes=[pltpu.VMEM((n_blk, b, LANES), f32)],
        compiler_params=_params(1),
        name="sample_branches",
    )(z, lru_state_t, h0, p["lru_conv_w"], p["lru_conv_b"], p["lru_wab"], p["lru_ba"], p["lru_bx"],
      p["lru_lambda"], z, z, cc_state_t, p["cc_dw_w"], p["cc_dw_b"], p["cc_ln_g"], p["cc_ln_b"])


def _kv_rows(z, rows):
    out = []
    for g in range(N_GROUPS):
        zz = z[z.shape[0] - rows[g]:]
        k = zz[:, COL_K + g * GROUP_WIDTH:COL_K + (g + 1) * GROUP_WIDTH]
        v = zz[:, COL_V + g * GROUP_WIDTH:COL_V + (g + 1) * GROUP_WIDTH]
        out.append(jnp.stack([k, v], axis=1).reshape(rows[g], 2, HEADS_PER_GROUP, HEAD_DIM))
    return out


def kernel(x_prompt, x_sample, cache_kv_w128, cache_kv_w512, cache_kv_w2048, state_lru_h, state_lru_conv,
           state_cconv, norm1_g, w_in, b_in, lru_conv_w, lru_conv_b, lru_wa, lru_ba, lru_wx, lru_bx,
           lru_lambda, cc_dw_w, cc_dw_b, cc_ln_g, cc_ln_b, w_proj_a, w_proj_b, w_proj_c, w_out,
           norm2_g, w_up, w_down, final_g):
    depth = w_in.shape[0]
    batch, seq, _ = x_prompt.shape
    dec_batch, dec_seq, _ = x_sample.shape
    assert batch == 1 and dec_seq == 1
    row = lambda a: a.reshape(depth, 1, a.shape[-1])
    p = {
        "norm1_g": row(norm1_g), "w_in": w_in.astype(bf16), "b_in": row(b_in),
        "lru_conv_w": lru_conv_w, "lru_conv_b": row(lru_conv_b),
        "lru_wab": jnp.concatenate([lru_wa, lru_wx], axis=-1).astype(bf16),
        "lru_ba": row(lru_ba), "lru_bx": row(lru_bx), "lru_lambda": row(lru_lambda),
        "cc_dw_w": cc_dw_w, "cc_dw_b": row(cc_dw_b), "cc_ln_g": row(cc_ln_g), "cc_ln_b": row(cc_ln_b),
        "w_proj_a": w_proj_a.astype(bf16), "w_proj_b": w_proj_b.astype(bf16), "w_proj_c": w_proj_c.astype(bf16),
        "w_out": w_out.astype(bf16), "norm2_g": row(norm2_g),
        "w_up": w_up.astype(bf16), "w_down": w_down.astype(bf16),
    }
    fg = final_g.reshape(1, D_MODEL)
    caches = (cache_kv_w128, cache_kv_w512, cache_kv_w2048)
    lru_state_t = jnp.swapaxes(state_lru_conv, 1, 2)
    cc_state_t = jnp.swapaxes(state_cconv, 1, 2)

    xp = x_prompt.reshape(seq, D_MODEL)
    xs = x_sample.reshape(dec_batch, D_MODEL)
    p_states, s_states = [], []
    tm = 1024
    for l in range(depth):
        last = l == depth - 1
        z = norm_matmul(xp, p["norm1_g"], p["w_in"], p["b_in"], l, tm=tm, tn=1024)
        group_outs = []
        for g in range(N_GROUPS):
            qd, kd, vd = split_classes(z, g, tm=tm)
            group_outs.append(band_attention(qd, kd, vd, g))
        attn = merge_groups(group_outs, tm=tm)
        hs, h_last = lru_prompt(z, p, l, tb=512)
        conf_tb = 256
        cv, u_last = conformer_prompt(z, p, l, tb=conf_tb)
        merged = branch_merge(attn, hs, cv, z, p, l, tm=tm, tn=512)
        xp1 = residual_matmul(merged, p["w_out"], xp, l, tm=tm, tn=512)
        kv = _kv_rows(z, [min(w, seq) for w in GROUP_WINDOWS])
        p_states.append((kv[0][None], kv[1][None], kv[2][None], h_last,
                         z[seq - (RNN_CONV_WIDTH - 1):, COL_XR:COL_XR + D_RNN][None],
                         u_last[-1, CONF_HIST - (CONF_KERNEL - 1):][None]))
        xp = mlp(xp1, p, fg, l, tm=512, tf=1024, final_norm=last)

        zs = norm_matmul(xs, p["norm1_g"], p["w_in"], p["b_in"], l, tm=dec_batch, tn=1024)
        qkv = zs[:, :COL_XR].reshape(dec_batch, 3, N_GROUPS, HEADS_PER_GROUP, HEAD_DIM)
        attn_s = sample_attention(qkv, caches, l, bt=8).reshape(dec_batch, GROUP_WIDTH).astype(bf16)
        hs_s, h_new, u_s, cv_s = sample_branches(zs, lru_state_t, state_lru_h, cc_state_t, p, l)
        merged_s = branch_merge(attn_s, hs_s, cv_s, zs, p, l, tm=dec_batch, tn=512)
        xs1 = residual_matmul(merged_s, p["w_out"], xs, l, tm=dec_batch, tn=512)
        kv_s = _kv_rows(zs, [dec_batch] * N_GROUPS)
        s_states.append(tuple(k[:, None] for k in kv_s) + (
            h_new,
            jnp.concatenate([state_lru_conv[l][:, 1:], zs[:, None, COL_XR:COL_XR + D_RNN]], axis=1),
            jnp.concatenate([state_cconv[l][:, 1:], u_s[:, None]], axis=1)))
        xs = mlp(xs1, p, fg, l, tm=dec_batch, tf=1024, final_norm=last)

    stack = lambda states, i: jnp.stack([st[i] for st in states], axis=0)
    return (xp.reshape(batch, seq, D_MODEL), xs.reshape(dec_batch, dec_seq, D_MODEL),
            stack(p_states, 0), stack(p_states, 1), stack(p_states, 2),
            stack(p_states, 3), stack(p_states, 4), stack(p_states, 5),
            stack(s_states, 0), stack(s_states, 1), stack(s_states, 2),
            stack(s_states, 3), stack(s_states, 4), stack(s_states, 5))
```

```python
import functools

import numpy as np
import jax
import jax.numpy as jnp
from jax import lax
from jax.experimental import pallas as pl
from jax.experimental.pallas import tpu as pltpu

f32 = jnp.float32
bf16 = jnp.bfloat16

D_MODEL = 2048
HEAD_DIM = 128
N_GROUPS = 3
GROUP_WINDOWS = (128, 512, 2048)
GROUP_DILATIONS = (1, 4, 16)
HEADS_PER_GROUP = 4
GROUP_WIDTH = HEADS_PER_GROUP * HEAD_DIM
N_ATTN_HEADS = N_GROUPS * HEADS_PER_GROUP
KEYS_PER_QUERY = 128
ATTN_WIDTH = N_ATTN_HEADS * HEAD_DIM
D_RNN = 1536
RNN_BLOCKS = 12
RNN_BLOCK = 128
RNN_CONV_WIDTH = 4
LRU_C = 8.0
D_CONV = 1536
CONF_KERNEL = 31
D_FF = 4 * D_MODEL
NORM_EPS = 1e-6
NEG_INF = -1e30

COL_Q = 0
COL_K = ATTN_WIDTH
COL_V = 2 * ATTN_WIDTH
COL_XR = 3 * ATTN_WIDTH
COL_GLU_A = COL_XR + D_RNN
COL_GLU_B = COL_GLU_A + D_CONV
COL_GATE = COL_GLU_B + D_CONV
N_IN = COL_GATE + 3 * D_MODEL

LANES = 128
SUBLANES = 8
VMEM_LIMIT = 56 * 1024 * 1024


def _alibi_slopes():
    h = np.arange(1, N_ATTN_HEADS + 1, dtype=np.float32)
    return np.power(np.float32(2.0), -8.0 * h / N_ATTN_HEADS).astype(np.float32)


def _params(n_axes):
    return pltpu.CompilerParams(dimension_semantics=("arbitrary",) * n_axes, vmem_limit_bytes=VMEM_LIMIT)


def _rmsnorm(x, g):
    ms = jnp.mean(x * x, axis=-1, keepdims=True)
    return x * lax.rsqrt(ms + NORM_EPS) * g


def _norm_matmul_kernel(x_ref, g_ref, w_ref, b_ref, o_ref, h_ref):
    @pl.when(pl.program_id(1) == 0)
    def _():
        h_ref[...] = _rmsnorm(x_ref[...], g_ref[...]).astype(bf16)

    o_ref[...] = jnp.dot(h_ref[...], w_ref[...], preferred_element_type=f32) + b_ref[...]


def norm_matmul(x, g, w, b, layer, tm, tn, col0=0):
    m, d = x.shape
    n = w.shape[-1] - col0
    c0 = col0 // tn
    assert c0 * tn == col0
    return pl.pallas_call(
        _norm_matmul_kernel,
        grid=(m // tm, n // tn),
        in_specs=[pl.BlockSpec((tm, d), lambda i, j: (i, 0)),
                  pl.BlockSpec((None, 1, d), lambda i, j: (layer, 0, 0)),
                  pl.BlockSpec((None, d, tn), lambda i, j: (layer, 0, c0 + j)),
                  pl.BlockSpec((None, 1, tn), lambda i, j: (layer, 0, c0 + j))],
        out_specs=pl.BlockSpec((tm, tn), lambda i, j: (i, j)),
        out_shape=jax.ShapeDtypeStruct((m, n), f32),
        scratch_shapes=[pltpu.VMEM((tm, d), bf16)],
        compiler_params=_params(2),
        name="norm_matmul",
    )(x, g, w, b)


N_QKV_BLOCKS = 3 * N_GROUPS


def _inproj_qkv_kernel(x_ref, g_ref, w_ref, b_ref, *refs, tm):
    class_refs = refs[:N_QKV_BLOCKS]
    kv_ref, h_ref, stage = refs[N_QKV_BLOCKS:]
    j = pl.program_id(1)

    @pl.when(j == 0)
    def _():
        h_ref[...] = _rmsnorm(x_ref[...], g_ref[...]).astype(bf16)

    acc = jnp.dot(h_ref[...], w_ref[...], preferred_element_type=f32) + b_ref[...]
    kv_ref[...] = acc
    for h in range(HEADS_PER_GROUP):
        stage[h] = acc[:, h * HEAD_DIM:(h + 1) * HEAD_DIM]

    for blk in range(N_QKV_BLOCKS):
        dil = GROUP_DILATIONS[blk % N_GROUPS]
        scale = HEAD_DIM ** -0.5 if blk < N_GROUPS else None
        n = tm // dil

        @pl.when(j == blk)
        def _(dst=class_refs[blk], dil=dil, scale=scale, n=n):
            for r in range(dil):
                val = jnp.concatenate([stage[h, pl.ds(r, n, stride=dil), :] for h in range(HEADS_PER_GROUP)], axis=-1)
                if scale is not None:
                    val = val * scale
                dst[r] = val.astype(bf16)


def inproj_qkv(x, g, w, b, layer, tm):
    t, d = x.shape
    class_specs, class_shapes = [], []
    for blk in range(N_QKV_BLOCKS):
        dil = GROUP_DILATIONS[blk % N_GROUPS]
        class_specs.append(pl.BlockSpec((dil, tm // dil, GROUP_WIDTH), lambda i, j: (0, i, 0)))
        class_shapes.append(jax.ShapeDtypeStruct((dil, t // dil, GROUP_WIDTH), bf16))
    outs = pl.pallas_call(
        functools.partial(_inproj_qkv_kernel, tm=tm),
        grid=(t // tm, N_QKV_BLOCKS),
        in_specs=[pl.BlockSpec((tm, d), lambda i, j: (i, 0)),
                  pl.BlockSpec((None, 1, d), lambda i, j: (layer, 0, 0)),
                  pl.BlockSpec((None, d, GROUP_WIDTH), lambda i, j: (layer, 0, j)),
                  pl.BlockSpec((None, 1, GROUP_WIDTH), lambda i, j: (layer, 0, j))],
        out_specs=class_specs + [pl.BlockSpec((tm, GROUP_WIDTH), lambda i, j: (i, jnp.maximum(j - N_GROUPS, 0)))],
        out_shape=class_shapes + [jax.ShapeDtypeStruct((t, 2 * ATTN_WIDTH), f32)],
        scratch_shapes=[pltpu.VMEM((tm, d), bf16), pltpu.VMEM((HEADS_PER_GROUP, tm, HEAD_DIM), f32)],
        compiler_params=_params(2),
        name="inproj_qkv",
    )(x, g, w, b)
    return outs[:N_QKV_BLOCKS], outs[N_QKV_BLOCKS]


def _band_attn_kernel(q_ref, kp_ref, kc_ref, vp_ref, vc_ref, o_ref, lse_ref, *, dil, slopes, chunks):
    c = pl.program_id(1)
    kq = KEYS_PER_QUERY
    shape = (kq, 2 * kq)
    row = lax.broadcasted_iota(jnp.int32, shape, 0)
    col = lax.broadcasted_iota(jnp.int32, shape, 1)
    steps = row + kq - col
    in_band = (steps >= 0) & (steps <= kq)
    first_valid_col = jnp.where(c > 0, 0, kq)
    in_band_first = in_band & (col >= first_valid_col)
    dist = steps.astype(f32) * float(dil)
    for u in range(chunks):
        rows = slice(u * kq, (u + 1) * kq)
        back = slice((u - 1) * kq, u * kq)
        valid = in_band_first if u == 0 else in_band
        for h in range(HEADS_PER_GROUP):
            sl = slice(h * HEAD_DIM, (h + 1) * HEAD_DIM)
            k_back = kp_ref[:, sl] if u == 0 else kc_ref[back, sl]
            v_back = vp_ref[:, sl] if u == 0 else vc_ref[back, sl]
            k2 = jnp.concatenate([k_back, kc_ref[rows, sl]], axis=0)
            v2 = jnp.concatenate([v_back, vc_ref[rows, sl]], axis=0)
            s = lax.dot_general(q_ref[rows, sl], k2, (((1,), (1,)), ((), ())), preferred_element_type=f32)
            s = jnp.where(valid, s - slopes[h] * dist, NEG_INF)
            m = jnp.max(s, axis=-1, keepdims=True)
            p = jnp.exp(s - m)
            l = jnp.sum(p, axis=-1, keepdims=True)
            o_ref[rows, sl] = jnp.dot(p.astype(bf16), v2, preferred_element_type=f32) / l
            lse_ref[rows, sl] = jnp.broadcast_to(m + jnp.log(l), (kq, HEAD_DIM))


def band_attention(qd, kd, vd, group, chunks):
    dil, length, _ = qd.shape
    slopes = tuple(float(s) for s in _alibi_slopes()[group * HEADS_PER_GROUP:(group + 1) * HEADS_PER_GROUP])
    cur = pl.BlockSpec((None, chunks * KEYS_PER_QUERY, GROUP_WIDTH), lambda r, c: (r, c, 0))
    prev = pl.BlockSpec((None, KEYS_PER_QUERY, GROUP_WIDTH), lambda r, c: (r, jnp.maximum(c * chunks - 1, 0), 0))
    out_sds = jax.ShapeDtypeStruct((dil, length, GROUP_WIDTH), f32)
    return pl.pallas_call(
        functools.partial(_band_attn_kernel, dil=dil, slopes=slopes, chunks=chunks),
        grid=(dil, length // (chunks * KEYS_PER_QUERY)),
        in_specs=[cur, prev, cur, prev, cur],
        out_specs=[cur, cur],
        out_shape=[out_sds, out_sds],
        compiler_params=_params(2),
        name=f"band_attention_g{group}",
    )(qd, kd, kd, vd, vd)


def _merge_groups_kernel(o0_ref, l0_ref, o1_ref, l1_ref, o2_ref, l2_ref, out_ref, so1, sl1, so2, sl2, *, tm):
    for src, dst, dil in ((o1_ref, so1, GROUP_DILATIONS[1]), (l1_ref, sl1, GROUP_DILATIONS[1]),
                          (o2_ref, so2, GROUP_DILATIONS[2]), (l2_ref, sl2, GROUP_DILATIONS[2])):
        n = tm // dil
        for r in range(dil):
            for h in range(HEADS_PER_GROUP):
                dst[h, pl.ds(r, n, stride=dil), :] = src[r, :, h * HEAD_DIM:(h + 1) * HEAD_DIM]
    for h in range(HEADS_PER_GROUP):
        sl = slice(h * HEAD_DIM, (h + 1) * HEAD_DIM)
        lses = (l0_ref[0, :, sl], sl1[h], sl2[h])
        outs = (o0_ref[0, :, sl], so1[h], so2[h])
        m = jnp.maximum(jnp.maximum(lses[0], lses[1]), lses[2])
        e = [jnp.exp(l - m) for l in lses]
        num = e[0] * outs[0] + e[1] * outs[1] + e[2] * outs[2]
        out_ref[:, sl] = (num / (e[0] + e[1] + e[2])).astype(bf16)


def merge_groups(group_outs, tm):
    t = group_outs[0][0].shape[1]
    specs, args = [], []
    for g, (o, lse) in enumerate(group_outs):
        dil = GROUP_DILATIONS[g]
        spec = pl.BlockSpec((dil, tm // dil, GROUP_WIDTH), lambda i: (0, i, 0))
        specs += [spec, spec]
        args += [o, lse]
    scr = pltpu.VMEM((HEADS_PER_GROUP, tm, HEAD_DIM), f32)
    return pl.pallas_call(
        functools.partial(_merge_groups_kernel, tm=tm),
        grid=(t // tm,),
        in_specs=specs,
        out_specs=pl.BlockSpec((tm, GROUP_WIDTH), lambda i: (i, 0)),
        out_shape=jax.ShapeDtypeStruct((t, GROUP_WIDTH), bf16),
        scratch_shapes=[scr] * 4,
        compiler_params=_params(1),
        name="merge_groups",
    )(*args)


def _lru_coeffs(xc, wab, ba, bx, lam):
    gates = jnp.dot(xc.astype(bf16), wab, preferred_element_type=f32)
    r = jax.nn.sigmoid(gates[:, :RNN_BLOCK] + ba)
    i = jax.nn.sigmoid(gates[:, RNN_BLOCK:] + bx)
    softplus_neg_lam = jnp.maximum(-lam, 0.0) + jnp.log(1.0 + jnp.exp(-jnp.abs(lam)))
    log_a = -LRU_C * r * softplus_neg_lam
    a = jnp.exp(log_a)
    th = jnp.tanh(log_a)
    one_minus_a2 = -2.0 * th / (1.0 - th)
    b = jnp.sqrt(one_minus_a2) * (i * xc)
    return a, b


BF16_ROWS = 2 * SUBLANES


def _lru_prompt_kernel(x_ref, cw_ref, cb_ref, wab_ref, ba_ref, bx_ref, lam_ref, hs_ref, hl_ref,
                       xbuf, a_scr, b_scr, hcar, *, tb):
    @pl.when(pl.program_id(0) == 0)
    def _():
        xbuf[0:SUBLANES] = jnp.zeros((SUBLANES, D_RNN), f32)
        hcar[...] = jnp.zeros_like(hcar)

    xbuf[SUBLANES:SUBLANES + tb] = x_ref[...]
    sub = lax.broadcasted_iota(jnp.int32, (tb // SUBLANES, SUBLANES, RNN_BLOCK), 1)
    for n in range(RNN_BLOCKS):
        sl = slice(n * RNN_BLOCK, (n + 1) * RNN_BLOCK)
        cw = cw_ref[:, sl]
        xc = cb_ref[:, sl] + cw[3:4] * xbuf[pl.ds(SUBLANES, tb), sl]
        for back in range(1, RNN_CONV_WIDTH):
            xc = xc + cw[3 - back:4 - back] * xbuf[pl.ds(SUBLANES - back, tb), sl]
        a, b = _lru_coeffs(xc, wab_ref[n], ba_ref[:, sl], bx_ref[:, sl], lam_ref[:, sl])
        a3 = a.reshape(tb // SUBLANES, SUBLANES, RNN_BLOCK)
        b3 = b.reshape(tb // SUBLANES, SUBLANES, RNN_BLOCK)
        for s in (1, 2, 4):
            keep = sub >= s
            a_prev = pltpu.roll(a3, s, axis=1)
            b_prev = pltpu.roll(b3, s, axis=1)
            b3 = jnp.where(keep, a3 * b_prev + b3, b3)
            a3 = jnp.where(keep, a3 * a_prev, a3)
        a_scr[:, sl] = a3.reshape(tb, RNN_BLOCK)
        b_scr[:, sl] = b3.reshape(tb, RNN_BLOCK)

    def body(g, h_prev):
        base = pl.multiple_of(g * BF16_ROWS, BF16_ROWS)
        lo = pl.ds(base, SUBLANES)
        hi = pl.ds(base + SUBLANES, SUBLANES)
        h_lo = a_scr[lo, :] * h_prev + b_scr[lo, :]
        h_hi = a_scr[hi, :] * h_lo[SUBLANES - 1:SUBLANES, :] + b_scr[hi, :]
        hs_ref[pl.ds(base, BF16_ROWS), :] = jnp.concatenate([h_lo, h_hi], axis=0).astype(hs_ref.dtype)
        return h_hi[SUBLANES - 1:SUBLANES, :]

    h_last = lax.fori_loop(0, tb // BF16_ROWS, body, hcar[0:1, :])
    hcar[0:1, :] = h_last
    hl_ref[...] = h_last
    xbuf[0:SUBLANES] = xbuf[tb:tb + SUBLANES]


def lru_prompt(z, col0, p, layer, tb):
    t = z.shape[0]
    vec = lambda rows: pl.BlockSpec((None, rows, D_RNN), lambda i: (layer, 0, 0))
    return pl.pallas_call(
        functools.partial(_lru_prompt_kernel, tb=tb),
        grid=(t // tb,),
        in_specs=[pl.BlockSpec((tb, D_RNN), lambda i, c=(COL_XR - col0) // D_RNN: (i, c)),
                  vec(RNN_CONV_WIDTH), vec(1),
                  pl.BlockSpec((None, RNN_BLOCKS, RNN_BLOCK, 2 * RNN_BLOCK), lambda i: (layer, 0, 0, 0)),
                  vec(1), vec(1), vec(1)],
        out_specs=[pl.BlockSpec((tb, D_RNN), lambda i: (i, 0)),
                   pl.BlockSpec((1, D_RNN), lambda i: (0, 0))],
        out_shape=[jax.ShapeDtypeStruct((t, D_RNN), bf16), jax.ShapeDtypeStruct((1, D_RNN), f32)],
        scratch_shapes=[pltpu.VMEM((tb + SUBLANES, D_RNN), f32), pltpu.VMEM((tb, D_RNN), f32),
                        pltpu.VMEM((tb, D_RNN), f32), pltpu.VMEM((SUBLANES, D_RNN), f32)],
        compiler_params=_params(1),
        name="lru_prompt",
    )(z, p["lru_conv_w"], p["lru_conv_b"], p["lru_wab"], p["lru_ba"], p["lru_bx"], p["lru_lambda"])


CONF_HIST = 32
CONF_ACC_ROWS = 256


def _layernorm_swish_store(y_scr, g_ref, b_ref, cv_ref):
    n_blk = D_CONV // LANES
    total = y_scr[0].sum(axis=-1, keepdims=True)
    for k in range(1, n_blk):
        total = total + y_scr[k].sum(axis=-1, keepdims=True)
    mu = total / D_CONV
    sq = jnp.square(y_scr[0] - mu).sum(axis=-1, keepdims=True)
    for k in range(1, n_blk):
        sq = sq + jnp.square(y_scr[k] - mu).sum(axis=-1, keepdims=True)
    inv = lax.rsqrt(sq / D_CONV + NORM_EPS)
    for k in range(n_blk):
        sl = slice(k * LANES, (k + 1) * LANES)
        y = (y_scr[k] - mu) * inv * g_ref[:, sl] + b_ref[:, sl]
        cv_ref[:, sl] = (y * jax.nn.sigmoid(y)).astype(cv_ref.dtype)


def _conformer_prompt_kernel(ga_ref, gb_ref, w_ref, b_ref, lng_ref, lnb_ref, cv_ref, ulast_ref,
                             ubuf, hist, y_scr, *, tb):
    i = pl.program_id(0)
    n = pl.program_id(1)

    @pl.when(i == 0)
    def _():
        hist[n] = jnp.zeros((CONF_HIST, LANES), f32)

    u = ga_ref[...] * jax.nn.sigmoid(gb_ref[...])
    ubuf[0:CONF_HIST] = hist[n]
    ubuf[CONF_HIST:CONF_HIST + tb] = u
    hist[n] = u[tb - CONF_HIST:tb]
    ulast_ref[...] = u[tb - CONF_HIST:tb]
    first = CONF_HIST - (CONF_KERNEL - 1)
    for r0 in range(0, tb, CONF_ACC_ROWS):
        acc = b_ref[...] + w_ref[0:1, :] * ubuf[pl.ds(first + r0, CONF_ACC_ROWS), :]
        for j in range(1, CONF_KERNEL):
            acc = acc + w_ref[j:j + 1, :] * ubuf[pl.ds(first + r0 + j, CONF_ACC_ROWS), :]
        y_scr[n, r0:r0 + CONF_ACC_ROWS, :] = acc

    @pl.when(n == D_CONV // LANES - 1)
    def _():
        _layernorm_swish_store(y_scr, lng_ref, lnb_ref, cv_ref)


def conformer_prompt(z, col0, p, layer, tb):
    t = z.shape[0]
    n_blk = D_CONV // LANES
    assert tb % CONF_ACC_ROWS == 0
    col = lambda base: pl.BlockSpec((tb, LANES), lambda i, n, c=(base - col0) // LANES: (i, c + n))
    return pl.pallas_call(
        functools.partial(_conformer_prompt_kernel, tb=tb),
        grid=(t // tb, n_blk),
        in_specs=[col(COL_GLU_A), col(COL_GLU_B),
                  pl.BlockSpec((None, CONF_KERNEL, LANES), lambda i, n: (layer, 0, n)),
                  pl.BlockSpec((None, 1, LANES), lambda i, n: (layer, 0, n)),
                  pl.BlockSpec((None, 1, D_CONV), lambda i, n: (layer, 0, 0)),
                  pl.BlockSpec((None, 1, D_CONV), lambda i, n: (layer, 0, 0))],
        out_specs=[pl.BlockSpec((tb, D_CONV), lambda i, n: (i, 0)),
                   pl.BlockSpec((None, CONF_HIST, LANES), lambda i, n: (i, 0, n))],
        out_shape=[jax.ShapeDtypeStruct((t, D_CONV), bf16),
                   jax.ShapeDtypeStruct((t // tb, CONF_HIST, D_CONV), f32)],
        scratch_shapes=[pltpu.VMEM((CONF_HIST + tb, LANES), f32), pltpu.VMEM((n_blk, CONF_HIST, LANES), f32),
                        pltpu.VMEM((n_blk, tb, LANES), f32)],
        compiler_params=_params(2),
        name="conformer_prompt",
    )(z, z, p["cc_dw_w"], p["cc_dw_b"], p["cc_ln_g"], p["cc_ln_b"])


def _branch_merge_kernel(at_ref, hs_ref, cv_ref, wa_ref, wb_ref, wc_ref, ga_ref, gb_ref, gc_ref, o_ref):
    ya = jnp.dot(at_ref[...], wa_ref[...], preferred_element_type=f32)
    yb = jnp.dot(hs_ref[...], wb_ref[...], preferred_element_type=f32)
    yc = jnp.dot(cv_ref[...], wc_ref[...], preferred_element_type=f32)
    merged = (jax.nn.sigmoid(ga_ref[...]) * ya + jax.nn.sigmoid(gb_ref[...]) * yb
              + jax.nn.sigmoid(gc_ref[...]) * yc)
    o_ref[...] = merged.astype(o_ref.dtype)


def branch_merge(attn, hs, cv, z, col0, p, layer, tm, tn):
    m = attn.shape[0]
    act = lambda width: pl.BlockSpec((tm, width), lambda i, j: (i, 0))
    wgt = lambda rows: pl.BlockSpec((None, rows, tn), lambda i, j: (layer, 0, j))
    gate = lambda br: pl.BlockSpec((tm, tn), lambda i, j, c=(COL_GATE - col0 + br * D_MODEL) // tn: (i, c + j))
    return pl.pallas_call(
        _branch_merge_kernel,
        grid=(m // tm, D_MODEL // tn),
        in_specs=[act(GROUP_WIDTH), act(D_RNN), act(D_CONV), wgt(GROUP_WIDTH), wgt(D_RNN), wgt(D_CONV),
                  gate(0), gate(1), gate(2)],
        out_specs=pl.BlockSpec((tm, tn), lambda i, j: (i, j)),
        out_shape=jax.ShapeDtypeStruct((m, D_MODEL), bf16),
        compiler_params=_params(2),
        name="branch_merge",
    )(attn, hs, cv, p["w_proj_a"], p["w_proj_b"], p["w_proj_c"], z, z, z)


def _residual_matmul_kernel(a_ref, w_ref, x_ref, o_ref):
    o_ref[...] = x_ref[...] + jnp.dot(a_ref[...], w_ref[...], preferred_element_type=f32)


def residual_matmul(a, w, x, layer, tm, tn):
    m, k = a.shape
    n = w.shape[-1]
    return pl.pallas_call(
        _residual_matmul_kernel,
        grid=(m // tm, n // tn),
        in_specs=[pl.BlockSpec((tm, k), lambda i, j: (i, 0)),
                  pl.BlockSpec((None, k, tn), lambda i, j: (layer, 0, j)),
                  pl.BlockSpec((tm, tn), lambda i, j: (i, j))],
        out_specs=pl.BlockSpec((tm, tn), lambda i, j: (i, j)),
        out_shape=jax.ShapeDtypeStruct((m, n), f32),
        compiler_params=_params(2),
        name="residual_matmul",
    )(a, w, x)


def _mlp_kernel(x_ref, g_ref, wu_ref, wd_ref, fg_ref, o_ref, h_ref, acc_ref, *, final_norm):
    f = pl.program_id(1)

    @pl.when(f == 0)
    def _():
        h_ref[...] = _rmsnorm(x_ref[...], g_ref[...]).astype(bf16)
        acc_ref[...] = jnp.zeros_like(acc_ref)

    up = jnp.dot(h_ref[...], wu_ref[...], preferred_element_type=f32)
    act = jnp.square(jnp.maximum(up, 0.0)).astype(bf16)
    acc_ref[...] += jnp.dot(act, wd_ref[...], preferred_element_type=f32)

    @pl.when(f == pl.num_programs(1) - 1)
    def _():
        y = x_ref[...] + acc_ref[...]
        if final_norm:
            y = _rmsnorm(y, fg_ref[...])
        o_ref[...] = y


def mlp(x, p, final_g, layer, tm, tf, final_norm):
    m, d = x.shape
    return pl.pallas_call(
        functools.partial(_mlp_kernel, final_norm=final_norm),
        grid=(m // tm, D_FF // tf),
        in_specs=[pl.BlockSpec((tm, d), lambda i, f: (i, 0)),
                  pl.BlockSpec((None, 1, d), lambda i, f: (layer, 0, 0)),
                  pl.BlockSpec((None, d, tf), lambda i, f: (layer, 0, f)),
                  pl.BlockSpec((None, tf, d), lambda i, f: (layer, f, 0)),
                  pl.BlockSpec((1, d), lambda i, f: (0, 0))],
        out_specs=pl.BlockSpec((tm, d), lambda i, f: (i, 0)),
        out_shape=jax.ShapeDtypeStruct((m, d), f32),
        scratch_shapes=[pltpu.VMEM((tm, d), bf16), pltpu.VMEM((tm, d), f32)],
        compiler_params=_params(2),
        name="mlp",
    )(x, p["norm2_g"], p["w_up"], p["w_down"], final_g)


KV_ROW = 2 * HEADS_PER_GROUP


def _head_rows(c_ref, b, row):
    bt = c_ref.shape[0]
    flat = c_ref.reshape(bt * KEYS_PER_QUERY * KV_ROW, HEAD_DIM)
    return flat[pl.ds(b * KEYS_PER_QUERY * KV_ROW + row, KEYS_PER_QUERY, stride=KV_ROW), :]


def _sample_attn_kernel(q_ref, kv_ref, c0_ref, c1_ref, c2_ref, o_ref, *, slopes):
    bt = q_ref.shape[0]
    key_idx = lax.broadcasted_iota(jnp.int32, (KEYS_PER_QUERY, 1), 0)
    rows_back = (KEYS_PER_QUERY - key_idx).astype(f32)
    for b in range(bt):
        for h in range(HEADS_PER_GROUP):
            outs, lses = [], []
            for g, c_ref in enumerate((c0_ref, c1_ref, c2_ref)):
                bias = rows_back * (slopes[g * HEADS_PER_GROUP + h] * GROUP_DILATIONS[g])
                q = q_ref[b, g, h:h + 1, :] * (HEAD_DIM ** -0.5)
                k = _head_rows(c_ref, b, h)
                v = _head_rows(c_ref, b, HEADS_PER_GROUP + h)
                k_new = kv_ref[b, g, h:h + 1, :]
                v_new = kv_ref[b, g, HEADS_PER_GROUP + h:HEADS_PER_GROUP + h + 1, :]
                s_c = jnp.sum(k * q, axis=-1, keepdims=True) - bias
                s_n = jnp.sum(k_new * q, axis=-1, keepdims=True)
                m = jnp.maximum(jnp.max(s_c, axis=0, keepdims=True), s_n)
                p_c = jnp.exp(s_c - m)
                p_n = jnp.exp(s_n - m)
                l = jnp.sum(p_c, axis=0, keepdims=True) + p_n
                outs.append((jnp.sum(p_c * v, axis=0, keepdims=True) + p_n * v_new) / l)
                lses.append(m + jnp.log(l))
            m = jnp.maximum(jnp.maximum(lses[0], lses[1]), lses[2])
            e = [jnp.exp(l - m) for l in lses]
            o_ref[b, h:h + 1, :] = (e[0] * outs[0] + e[1] * outs[1] + e[2] * outs[2]) / (e[0] + e[1] + e[2])


def sample_attention(zs, caches, layer, bt):
    b = zs.shape[0]
    qkv = zs[:, :COL_XR].reshape(b, 3, N_GROUPS, HEADS_PER_GROUP, HEAD_DIM)
    q4 = qkv[:, 0]
    kv8 = jnp.concatenate([qkv[:, 1], qkv[:, 2]], axis=2)
    specs, args = [], []
    for g, cache in enumerate(caches):
        dil = GROUP_DILATIONS[g]
        depth, _, w = cache.shape[:3]
        assert w == KEYS_PER_QUERY * dil
        args.append(cache.reshape(depth, b, KEYS_PER_QUERY, dil * KV_ROW, HEAD_DIM))
        specs.append(pl.BlockSpec((None, bt, KEYS_PER_QUERY, KV_ROW, HEAD_DIM), lambda i: (layer, i, 0, 0, 0)))
    slopes = tuple(float(s) for s in _alibi_slopes())
    new_row = lambda rows: pl.BlockSpec((bt, N_GROUPS, rows, HEAD_DIM), lambda i: (i, 0, 0, 0))
    out = pl.pallas_call(
        functools.partial(_sample_attn_kernel, slopes=slopes),
        grid=(b // bt,),
        in_specs=[new_row(HEADS_PER_GROUP), new_row(KV_ROW)] + specs,
        out_specs=pl.BlockSpec((bt, HEADS_PER_GROUP, HEAD_DIM), lambda i: (i, 0, 0)),
        out_shape=jax.ShapeDtypeStruct((b, HEADS_PER_GROUP, HEAD_DIM), f32),
        compiler_params=_params(1),
        name="sample_attention",
    )(q4, kv8, *args)
    return out.reshape(b, GROUP_WIDTH)


def _sample_branches_kernel(xr_ref, lst_ref, h0_ref, cw_ref, cb_ref, wab_ref, ba_ref, bx_ref, lam_ref,
                            ga_ref, gb_ref, cst_ref, w_ref, b_ref, lng_ref, lnb_ref,
                            hs_ref, hn_ref, nlst_ref, ncst_ref, cv_ref, y_scr):
    n = pl.program_id(0)
    lru_past = RNN_CONV_WIDTH - 1
    conf_past = CONF_KERNEL - 1

    xr = xr_ref[...]
    cw = cw_ref[...]
    xc = cb_ref[...] + cw[lru_past:lru_past + 1] * xr + jnp.sum(lst_ref[...] * cw[0:lru_past][None], axis=1)
    a, b = _lru_coeffs(xc, wab_ref[...], ba_ref[...], bx_ref[...], lam_ref[...])
    h = a * h0_ref[...] + b
    hn_ref[...] = h
    hs_ref[...] = h.astype(hs_ref.dtype)
    nlst_ref[:, 0:lru_past - 1, :] = lst_ref[:, 1:lru_past, :]
    nlst_ref[:, lru_past - 1:lru_past, :] = xr[:, None, :]

    u = ga_ref[...] * jax.nn.sigmoid(gb_ref[...])
    w = w_ref[...]
    y_scr[n] = b_ref[...] + w[conf_past:conf_past + 1] * u + jnp.sum(cst_ref[...] * w[0:conf_past][None], axis=1)
    ncst_ref[:, 0:conf_past - 1, :] = cst_ref[:, 1:conf_past, :]
    ncst_ref[:, conf_past - 1:conf_past, :] = u[:, None, :]

    @pl.when(n == D_CONV // LANES - 1)
    def _():
        _layernorm_swish_store(y_scr, lng_ref, lnb_ref, cv_ref)


def sample_branches(z, lru_state, h0, cc_state, p, layer):
    b = z.shape[0]
    n_blk = D_CONV // LANES
    col = lambda base: pl.BlockSpec((b, LANES), lambda n, c=base // LANES: (0, c + n))
    vec = lambda rows: pl.BlockSpec((None, rows, LANES), lambda n: (layer, 0, n))
    full = pl.BlockSpec((None, 1, D_CONV), lambda n: (layer, 0, 0))
    blk = pl.BlockSpec((b, LANES), lambda n: (0, n))
    state_in = lambda rows: pl.BlockSpec((None, b, rows, LANES), lambda n: (layer, 0, 0, n))
    state_out = lambda rows: pl.BlockSpec((b, rows, LANES), lambda n: (0, 0, n))
    return pl.pallas_call(
        _sample_branches_kernel,
        grid=(n_blk,),
        in_specs=[col(COL_XR), state_in(RNN_CONV_WIDTH - 1),
                  pl.BlockSpec((None, b, LANES), lambda n: (layer, 0, n)),
                  vec(RNN_CONV_WIDTH), vec(1),
                  pl.BlockSpec((None, None, RNN_BLOCK, 2 * RNN_BLOCK), lambda n: (layer, n, 0, 0)),
                  vec(1), vec(1), vec(1),
                  col(COL_GLU_A), col(COL_GLU_B), state_in(CONF_KERNEL - 1),
                  vec(CONF_KERNEL), vec(1), full, full],
        out_specs=[blk, blk, state_out(RNN_CONV_WIDTH - 1), state_out(CONF_KERNEL - 1),
                   pl.BlockSpec((b, D_CONV), lambda n: (0, 0))],
        out_shape=[jax.ShapeDtypeStruct((b, D_RNN), bf16), jax.ShapeDtypeStruct((b, D_RNN), f32),
                   jax.ShapeDtypeStruct((b, RNN_CONV_WIDTH - 1, D_RNN), f32),
                   jax.ShapeDtypeStruct((b, CONF_KERNEL - 1, D_CONV), f32),
                   jax.ShapeDtypeStruct((b, D_CONV), bf16)],
        scratch_shapes=[pltpu.VMEM((n_blk, b, LANES), f32)],
        compiler_params=_params(1),
        name="sample_branches",
    )(z, lru_state, h0, p["lru_conv_w"], p["lru_conv_b"], p["lru_wab"], p["lru_ba"], p["lru_bx"],
      p["lru_lambda"], z, z, cc_state, p["cc_dw_w"], p["cc_dw_b"], p["cc_ln_g"], p["cc_ln_b"])


def _kv_rows(kv, rows):
    out = []
    for g in range(N_GROUPS):
        tail = kv[kv.shape[0] - rows[g]:]
        k = tail[:, g * GROUP_WIDTH:(g + 1) * GROUP_WIDTH]
        v = tail[:, ATTN_WIDTH + g * GROUP_WIDTH:ATTN_WIDTH + (g + 1) * GROUP_WIDTH]
        out.append(jnp.stack([k, v], axis=1).reshape(rows[g], 2, HEADS_PER_GROUP, HEAD_DIM))
    return out


TILES = dict(
    qkv_rows=512,
    proj_rows=1024, proj_cols=768,
    attn_chunks=4,
    merge_rows=1024,
    lru_rows=512,
    conf_rows=1024,
    branch_rows=1024, branch_cols=512,
    out_rows=1024, out_cols=512,
    mlp_rows=512, mlp_cols=1024,
    sample_cols=1024,
    sample_attn_rows=8,
)


def kernel(x_prompt, x_sample, cache_kv_w128, cache_kv_w512, cache_kv_w2048, state_lru_h, state_lru_conv,
           state_cconv, norm1_g, w_in, b_in, lru_conv_w, lru_conv_b, lru_wa, lru_ba, lru_wx, lru_bx,
           lru_lambda, cc_dw_w, cc_dw_b, cc_ln_g, cc_ln_b, w_proj_a, w_proj_b, w_proj_c, w_out,
           norm2_g, w_up, w_down, final_g):
    depth = w_in.shape[0]
    batch, seq, _ = x_prompt.shape
    dec_batch, dec_seq, _ = x_sample.shape
    assert batch == 1 and dec_seq == 1
    row = lambda a: a.reshape(depth, 1, a.shape[-1])
    p = {
        "norm1_g": row(norm1_g), "w_in": w_in.astype(bf16), "b_in": row(b_in),
        "lru_conv_w": lru_conv_w, "lru_conv_b": row(lru_conv_b),
        "lru_wab": jnp.concatenate([lru_wa, lru_wx], axis=-1).astype(bf16),
        "lru_ba": row(lru_ba), "lru_bx": row(lru_bx), "lru_lambda": row(lru_lambda),
        "cc_dw_w": cc_dw_w, "cc_dw_b": row(cc_dw_b), "cc_ln_g": row(cc_ln_g), "cc_ln_b": row(cc_ln_b),
        "w_proj_a": w_proj_a.astype(bf16), "w_proj_b": w_proj_b.astype(bf16), "w_proj_c": w_proj_c.astype(bf16),
        "w_out": w_out.astype(bf16), "norm2_g": row(norm2_g),
        "w_up": w_up.astype(bf16), "w_down": w_down.astype(bf16),
    }
    fg = final_g.reshape(1, D_MODEL)
    caches = (cache_kv_w128, cache_kv_w512, cache_kv_w2048)

    xp = x_prompt.reshape(seq, D_MODEL)
    xs = x_sample.reshape(dec_batch, D_MODEL)
    p_states, s_states = [], []
    t = TILES
    for l in range(depth):
        last = l == depth - 1
        classes, kv_p = inproj_qkv(xp, p["norm1_g"], p["w_in"], p["b_in"], l, tm=t["qkv_rows"])
        zr = norm_matmul(xp, p["norm1_g"], p["w_in"], p["b_in"], l, tm=t["proj_rows"], tn=t["proj_cols"], col0=COL_XR)
        group_outs = [band_attention(classes[g], classes[N_GROUPS + g], classes[2 * N_GROUPS + g], g,
                                     chunks=min(t["attn_chunks"], seq // GROUP_DILATIONS[g] // KEYS_PER_QUERY))
                      for g in range(N_GROUPS)]
        attn = merge_groups(group_outs, tm=t["merge_rows"])
        hs, h_last = lru_prompt(zr, COL_XR, p, l, tb=t["lru_rows"])
        cv, u_last = conformer_prompt(zr, COL_XR, p, l, tb=t["conf_rows"])
        merged = branch_merge(attn, hs, cv, zr, COL_XR, p, l, tm=t["branch_rows"], tn=t["branch_cols"])
        xp1 = residual_matmul(merged, p["w_out"], xp, l, tm=t["out_rows"], tn=t["out_cols"])
        kv = _kv_rows(kv_p, [min(w, seq) for w in GROUP_WINDOWS])
        p_states.append((kv[0][None], kv[1][None], kv[2][None], h_last,
                         zr[seq - (RNN_CONV_WIDTH - 1):, :D_RNN][None],
                         u_last[-1, CONF_HIST - (CONF_KERNEL - 1):][None]))
        xp = mlp(xp1, p, fg, l, tm=t["mlp_rows"], tf=t["mlp_cols"], final_norm=last)

        zs = norm_matmul(xs, p["norm1_g"], p["w_in"], p["b_in"], l, tm=dec_batch, tn=t["sample_cols"])
        attn_s = sample_attention(zs, caches, l, bt=t["sample_attn_rows"]).astype(bf16)
        hs_s, h_new, lru_conv_new, cconv_new, cv_s = sample_branches(zs, state_lru_conv, state_lru_h, state_cconv, p, l)
        merged_s = branch_merge(attn_s, hs_s, cv_s, zs, 0, p, l, tm=dec_batch, tn=t["branch_cols"])
        xs1 = residual_matmul(merged_s, p["w_out"], xs, l, tm=dec_batch, tn=t["out_cols"])
        kv_s = _kv_rows(zs[:, COL_K:COL_XR], [dec_batch] * N_GROUPS)
        s_states.append(tuple(k[:, None] for k in kv_s) + (h_new, lru_conv_new, cconv_new))
        xs = mlp(xs1, p, fg, l, tm=dec_batch, tf=t["mlp_cols"], final_norm=last)

    stack = lambda states, i: jnp.stack([st[i] for st in states], axis=0)
    return (xp.reshape(batch, seq, D_MODEL), xs.reshape(dec_batch, dec_seq, D_MODEL),
            stack(p_states, 0), stack(p_states, 1), stack(p_states, 2),
            stack(p_states, 3), stack(p_states, 4), stack(p_states, 5),
            stack(s_states, 0), stack(s_states, 1), stack(s_states, 2),
            stack(s_states, 3), stack(s_states, 4), stack(s_states, 5))
```
